```python
import math
import jax, jax.numpy as jnp
from jax import lax
import numpy as np

D_MODEL = 4096
BATCH = 1
SEQ = 8192
DEPTH = 2

W_A = D_MODEL // 2
W_B = D_MODEL - W_A
CHUNK = 128
G_A = W_A // 128
H_B = 8
DV = W_B // H_B
DK = DV // 2
N_GATES = 4
D_FF = 256 * ((8 * D_MODEL // 3 + 255) // 256)
N_EXPERTS = 8
TOP_K = 2
D_EXPERT = 3 * D_MODEL // 2
N_DENSE = (DEPTH + 1) // 2
N_MOE = DEPTH // 2
RMS_EPS = 1e-6
LN_EPS = 1e-5
COLS = [W_A, W_A, H_B * DK, H_B * DK, W_B, W_B, N_GATES * H_B]
IN_COLS = sum(COLS)
SPLIT_IDX = [sum(COLS[:i + 1]) for i in range(len(COLS) - 1)]

kernel_name = "hybrid_gmlp_mlstm_moe_encoder"


def rmsnorm(x, g):
    xf = x.astype(jnp.float32)
    y = xf * lax.rsqrt(jnp.mean(xf * xf, axis=-1, keepdims=True) + RMS_EPS)
    return (y * g.astype(jnp.float32)).astype(x.dtype)


def layernorm(x, g, b):
    xf = x.astype(jnp.float32)
    mu = jnp.mean(xf, axis=-1, keepdims=True)
    var = jnp.mean(jnp.square(xf - mu), axis=-1, keepdims=True)
    y = (xf - mu) * lax.rsqrt(var + LN_EPS)
    return (y * g.astype(jnp.float32) + b.astype(jnp.float32)).astype(x.dtype)


def swiglu(h, w_gate, w_up, w_down):
    return (jax.nn.silu(h @ w_gate) * (h @ w_up)) @ w_down


def mlstm_direction(q, k, v, i_pre, f_pre):
    B_, H_, S_, _ = q.shape
    nc = S_ // CHUNK
    qc = q.reshape(B_, H_, nc, CHUNK, DK)
    kc = k.reshape(B_, H_, nc, CHUNK, DK)
    vc = v.reshape(B_, H_, nc, CHUNK, DV)
    ic = i_pre.reshape(B_, H_, nc, CHUNK)
    b = jnp.cumsum(jax.nn.log_sigmoid(f_pre).reshape(B_, H_, nc, CHUNK), axis=-1)
    b_end = b[..., -1]

    a = ic + b_end[..., None] - b
    m_loc = jnp.max(a, axis=-1)
    w_end = jnp.exp(a - m_loc[..., None])
    C_loc = jnp.einsum('bhcs,bhcsk,bhcsv->bhckv', w_end, kc, vc)
    n_loc = jnp.einsum('bhcs,bhcsk->bhck', w_end, kc)

    def step(carry, xs):
        C, n, m = carry
        Cl, nl, ml, be = xs
        m_new = jnp.maximum(be + m, ml)
        dec = jnp.exp(be + m - m_new)
        inc = jnp.exp(ml - m_new)
        C_new = dec[..., None, None] * C + inc[..., None, None] * Cl
        n_new = dec[..., None] * n + inc[..., None] * nl
        return (C_new, n_new, m_new), (C, n, m)

    init = (jnp.zeros((B_, H_, DK, DV), jnp.float32),
            jnp.zeros((B_, H_, DK), jnp.float32),
            jnp.zeros((B_, H_), jnp.float32))
    lead = lambda t: jnp.moveaxis(t, 2, 0)
    _, (C_prev, n_prev, m_prev) = lax.scan(step, init, (lead(C_loc), lead(n_loc), lead(m_loc), lead(b_end)))
    C_prev = jnp.moveaxis(C_prev, 0, 2)
    n_prev = jnp.moveaxis(n_prev, 0, 2)
    m_prev = jnp.moveaxis(m_prev, 0, 2)

    mask = jnp.tril(jnp.ones((CHUNK, CHUNK), dtype=bool))
    dlog = jnp.where(mask, b[..., :, None] - b[..., None, :] + ic[..., None, :], -jnp.inf)
    m_inter = b + m_prev[..., None]
    m_t = jnp.maximum(m_inter, jnp.max(dlog, axis=-1))
    scores = jnp.einsum('bhctk,bhcsk->bhcts', qc, kc) * jnp.exp(dlog - m_t[..., None])
    w_inter = jnp.exp(m_inter - m_t)
    num = (jnp.einsum('bhcts,bhcsv->bhctv', scores, vc)
           + w_inter[..., None] * jnp.einsum('bhctk,bhckv->bhctv', qc, C_prev))
    den = jnp.sum(scores, axis=-1) + w_inter * jnp.einsum('bhctk,bhck->bhct', qc, n_prev)
    h = num / jnp.maximum(jnp.abs(den), jnp.exp(-m_t))[..., None]
    return h.reshape(B_, H_, S_, DV)


def token_mixers(h, w_in, gate_bias, ln_v_gain, ln_v_bias, w_s, b_s, mh_norm, w_out):
    B_, S_, _ = h.shape
    z = h @ w_in
    u_a, v_a, q, k, v_b, o, g = jnp.split(z, SPLIT_IDX, axis=-1)

    u_a = jax.nn.gelu(u_a)
    v_a = layernorm(jax.nn.gelu(v_a), ln_v_gain, ln_v_bias)
    nc = S_ // CHUNK
    v_a = v_a.reshape(B_, nc, CHUNK, G_A, W_A // G_A)
    mixed = jnp.einsum('gts,bcsgd->bctgd', w_s, v_a) + jnp.transpose(b_s)[:, :, None]
    out_a = u_a * mixed.reshape(B_, S_, W_A)

    heads = lambda t, d: jnp.transpose(t.reshape(B_, S_, H_B, d), (0, 2, 1, 3)).astype(jnp.float32)
    qh = heads(q, DK) * (DK ** -0.5)
    kh = heads(k, DK)
    vh = heads(v_b, DV)
    gates = g.astype(jnp.float32).reshape(B_, S_, N_GATES, H_B) + gate_bias.astype(jnp.float32)
    gates = jnp.transpose(gates, (0, 2, 3, 1))
    h_fwd = mlstm_direction(qh, kh, vh, gates[:, 0], gates[:, 1])
    flip = lambda t: jnp.flip(t, axis=2)
    h_bwd = flip(mlstm_direction(flip(qh), flip(kh), flip(vh), flip(gates[:, 2]), flip(gates[:, 3])))
    hb = h_fwd + h_bwd
    hb = hb * lax.rsqrt(jnp.mean(hb * hb, axis=-1, keepdims=True) + RMS_EPS)
    hb = jnp.transpose(hb, (0, 2, 1, 3)).reshape(B_, S_, W_B) * mh_norm.astype(jnp.float32)
    out_b = hb.astype(h.dtype) * jax.nn.sigmoid(o)

    return jnp.concatenate([out_a, out_b], axis=-1) @ w_out


def moe_swiglu(h, router, w_gate, w_up, w_down):
    B_, S_, D_ = h.shape
    h2 = h.reshape(B_ * S_, D_)
    logits = (h2 @ router).astype(jnp.float32)
    top_v, top_i = lax.top_k(logits, TOP_K)
    top_w = jax.nn.softmax(top_v, axis=-1)
    gates = jnp.sum(jax.nn.one_hot(top_i, N_EXPERTS, dtype=jnp.float32) * top_w[..., None], axis=1)
    gates = gates.astype(h.dtype)
    out = jnp.zeros_like(h2)
    for e in range(N_EXPERTS):
        out = out + gates[:, e:e + 1] * swiglu(h2, w_gate[e], w_up[e], w_down[e])
    return out.reshape(B_, S_, D_)


def setup_inputs(seed: int = 0) -> dict:
    key = jax.random.key(seed)
    ks = jax.random.split(key, 20)
    f32 = jnp.float32
    nrm = lambda k, shape, scale: jax.random.normal(k, shape, f32) * scale
    forget_bias = jnp.linspace(3.0, 6.0, H_B, dtype=f32)
    gb_noise = nrm(ks[3], (DEPTH, N_GATES, H_B), 0.1)
    gate_bias = gb_noise + jnp.stack([jnp.zeros((H_B,), f32), forget_bias,
                                      jnp.zeros((H_B,), f32), forget_bias])[None]
    return {
        "x": nrm(ks[0], (BATCH, SEQ, D_MODEL), 1.0),
        "norm_mix": 1.0 + nrm(ks[1], (DEPTH, D_MODEL), 0.02),
        "w_in": nrm(ks[2], (DEPTH, D_MODEL, IN_COLS), D_MODEL ** -0.5),
        "gate_bias": gate_bias,
        "ln_v_gain": 1.0 + nrm(ks[4], (DEPTH, W_A), 0.02),
        "ln_v_bias": nrm(ks[5], (DEPTH, W_A), 0.02),
        "w_s": nrm(ks[6], (DEPTH, G_A, CHUNK, CHUNK), CHUNK ** -0.5),
        "b_s": 1.0 + nrm(ks[7], (DEPTH, G_A, CHUNK), 0.02),
        "mh_norm": 1.0 + nrm(ks[8], (DEPTH, W_B), 0.02),
        "w_out": nrm(ks[9], (DEPTH, D_MODEL, D_MODEL), D_MODEL ** -0.5),
        "norm_ffn": 1.0 + nrm(ks[10], (DEPTH, D_MODEL), 0.02),
        "ffn_w_gate": nrm(ks[11], (N_DENSE, D_MODEL, D_FF), D_MODEL ** -0.5),
        "ffn_w_up": nrm(ks[12], (N_DENSE, D_MODEL, D_FF), D_MODEL ** -0.5),
        "ffn_w_down": nrm(ks[13], (N_DENSE, D_FF, D_MODEL), D_FF ** -0.5),
        "router": nrm(ks[14], (N_MOE, D_MODEL, N_EXPERTS), D_MODEL ** -0.5),
        "moe_w_gate": nrm(ks[15], (N_MOE, N_EXPERTS, D_MODEL, D_EXPERT), D_MODEL ** -0.5),
        "moe_w_up": nrm(ks[16], (N_MOE, N_EXPERTS, D_MODEL, D_EXPERT), D_MODEL ** -0.5),
        "moe_w_down": nrm(ks[17], (N_MOE, N_EXPERTS, D_EXPERT, D_MODEL), D_EXPERT ** -0.5),
        "final_norm": 1.0 + nrm(ks[18], (D_MODEL,), 0.02),
    }


def reference(x, norm_mix, w_in, gate_bias, ln_v_gain, ln_v_bias, w_s, b_s, mh_norm, w_out,
              norm_ffn, ffn_w_gate, ffn_w_up, ffn_w_down, router, moe_w_gate, moe_w_up,
              moe_w_down, final_norm):
    for l in range(DEPTH):
        h = rmsnorm(x, norm_mix[l])
        x = x + token_mixers(h, w_in[l], gate_bias[l], ln_v_gain[l], ln_v_bias[l],
                             w_s[l], b_s[l], mh_norm[l], w_out[l])
        h = rmsnorm(x, norm_ffn[l])
        j = l // 2
        if l % 2 == 0:
            x = x + swiglu(h, ffn_w_gate[j], ffn_w_up[j], ffn_w_down[j])
        else:
            x = x + moe_swiglu(h, router[j], moe_w_gate[j], moe_w_up[j], moe_w_down[j])
    return rmsnorm(x, final_norm)
```

```python
import functools

import jax
import jax.numpy as jnp
from jax import lax
from jax.experimental import pallas as pl
from jax.experimental.pallas import tpu as pltpu

LANE = 128
CHUNK = 128
H_B = 8
N_GATES = 4
N_EXPERTS = 8
TOP_K = 2
RMS_EPS = 1e-6
LN_EPS = 1e-5
V7X_VMEM_LIMIT_BYTES = 56 * 1024 * 1024

MM_TM = 1024
MM_TN = 512
MOE_TM = 512
NORM_TM = 256

F32 = jnp.float32
BF16 = jnp.bfloat16


def _tile(dim, target, align):
    t = (min(target, dim) // align) * align
    while t >= align:
        if dim % t == 0:
            return t
        t -= align
    return dim


def _params(*sem):
    return pltpu.CompilerParams(dimension_semantics=sem, vmem_limit_bytes=V7X_VMEM_LIMIT_BYTES)


def _rmsnorm_kernel(x_ref, g_ref, o_ref):
    x = x_ref[...]
    ms = jnp.mean(x * x, axis=-1, keepdims=True)
    o_ref[...] = (x * lax.rsqrt(ms + RMS_EPS) * g_ref[...]).astype(o_ref.dtype)


def _rmsnorm(x, g, out_dtype):
    s, d = x.shape
    tm = _tile(s, NORM_TM, 8)
    return pl.pallas_call(
        _rmsnorm_kernel,
        out_shape=jax.ShapeDtypeStruct((s, d), out_dtype),
        grid=(s // tm,),
        in_specs=[pl.BlockSpec((tm, d), lambda i: (i, 0)),
                  pl.BlockSpec((1, d), lambda i: (0, 0))],
        out_specs=pl.BlockSpec((tm, d), lambda i: (i, 0)),
        compiler_params=_params("parallel"),
        name="rmsnorm",
    )(x, g.reshape(1, d).astype(F32))


def _mm_kernel(*refs, n_a, n_w, has_res, epilogue):
    a_refs = refs[:n_a]
    w_refs = refs[n_a:n_a + n_a * n_w]
    res_ref = refs[n_a + n_a * n_w] if has_res else None
    o_ref = refs[-1]
    accs = []
    for k in range(n_w):
        acc = None
        for a in range(n_a):
            p = jnp.dot(a_refs[a][...], w_refs[k * n_a + a][...], preferred_element_type=F32)
            acc = p if acc is None else acc + p
        accs.append(acc)
    res = res_ref[...] if has_res else None
    epilogue(accs, res, o_ref)


def _matmul(a_list, w_list, epilogue, out_dtype, res=None, tm=None, tn=None, name="matmul"):
    n_a, n_w = len(a_list), len(w_list)
    m, ka = a_list[0].shape
    n = w_list[0].shape[1]
    tm = _tile(m, tm or MM_TM, 8)
    tn = _tile(n, tn or MM_TN, LANE)
    in_specs = [pl.BlockSpec((tm, ka), lambda j, i: (i, 0)) for _ in range(n_a)]
    operands = list(a_list)
    for w in w_list:
        for a in range(n_a):
            in_specs.append(pl.BlockSpec((ka, tn), lambda j, i, a=a: (a, j)))
            operands.append(w)
    if res is not None:
        in_specs.append(pl.BlockSpec((tm, tn), lambda j, i: (i, j)))
        operands.append(res)
    kern = functools.partial(_mm_kernel, n_a=n_a, n_w=n_w, has_res=res is not None,
                             epilogue=functools.partial(epilogue, tn=tn))
    return pl.pallas_call(
        kern,
        out_shape=jax.ShapeDtypeStruct((m, n), out_dtype),
        grid=(n // tn, m // tm),
        in_specs=in_specs,
        out_specs=pl.BlockSpec((tm, tn), lambda j, i: (i, j)),
        compiler_params=_params("parallel", "parallel"),
        name=name,
    )(*operands)


def _epi_plain(accs, res, o_ref, tn):
    o_ref[...] = accs[0].astype(o_ref.dtype)


def _epi_residual(accs, res, o_ref, tn):
    o_ref[...] = (res + accs[0]).astype(o_ref.dtype)


def _epi_swiglu(accs, res, o_ref, tn):
    o_ref[...] = (jax.nn.silu(accs[0]) * accs[1]).astype(o_ref.dtype)


def _epi_gelu_prefix(accs, res, o_ref, tn, gelu_cols):
    j = pl.program_id(0)

    @pl.when(j * tn < gelu_cols)
    def _():
        o_ref[...] = jax.nn.gelu(accs[0]).astype(o_ref.dtype)

    @pl.when(j * tn >= gelu_cols)
    def _():
        o_ref[...] = accs[0].astype(o_ref.dtype)


def _gmlp_kernel(u_ref, v_ref, g_ref, b_ref, ws_ref, bs_ref, o_ref, *, n_groups, gw):
    v = v_ref[...].astype(F32)
    mu = jnp.mean(v, axis=-1, keepdims=True)
    vc = v - mu
    var = jnp.mean(vc * vc, axis=-1, keepdims=True)
    vn = (vc * lax.rsqrt(var + LN_EPS) * g_ref[...] + b_ref[...]).astype(BF16)
    for g in range(n_groups):
        cols = slice(g * gw, (g + 1) * gw)
        mixed = jnp.dot(ws_ref[g], vn[:, cols], preferred_element_type=F32) + bs_ref[:, g:g + 1]
        o_ref[:, cols] = (u_ref[:, cols].astype(F32) * mixed).astype(o_ref.dtype)


def _gmlp(z, ln_g, ln_b, w_s, b_s, w_a):
    s = z.shape[0]
    n_groups = w_s.shape[0]
    gw = w_a // n_groups
    kern = functools.partial(_gmlp_kernel, n_groups=n_groups, gw=gw)
    return pl.pallas_call(
        kern,
        out_shape=jax.ShapeDtypeStruct((s, w_a), BF16),
        grid=(s // CHUNK,),
        in_specs=[pl.BlockSpec((CHUNK, w_a), lambda c: (c, 0)),
                  pl.BlockSpec((CHUNK, w_a), lambda c: (c, 1)),
                  pl.BlockSpec((1, w_a), lambda c: (0, 0)),
                  pl.BlockSpec((1, w_a), lambda c: (0, 0)),
                  pl.BlockSpec((n_groups, CHUNK, CHUNK), lambda c: (0, 0, 0)),
                  pl.BlockSpec((CHUNK, n_groups), lambda c: (0, 0))],
        out_specs=pl.BlockSpec((CHUNK, w_a), lambda c: (c, 0)),
        compiler_params=_params("parallel"),
        name="gmlp_spatial_gate",
    )(z, z, ln_g.reshape(1, w_a).astype(F32), ln_b.reshape(1, w_a).astype(F32),
      w_s.astype(BF16), jnp.transpose(b_s).astype(F32))


def _log_sigmoid(x):
    return jnp.minimum(x, 0.0) - jnp.log1p(jnp.exp(-jnp.abs(x)))


def _split_dot(x, m):
    hi = x.astype(BF16)
    r1 = x - hi.astype(F32)
    mid = r1.astype(BF16)
    lo = (r1 - mid.astype(F32)).astype(BF16)
    dot = lambda p: jnp.dot(p, m, preferred_element_type=F32)
    return dot(hi) + dot(mid) + dot(lo)


def _gate_prep_kernel(g_ref, b_ref, o_ref):
    t = jnp.transpose(g_ref[...] + b_ref[...])
    i_f, f_f = t[0:H_B], t[H_B:2 * H_B]
    i_b, f_b = t[2 * H_B:3 * H_B], t[3 * H_B:4 * H_B]
    src = lax.broadcasted_iota(jnp.int32, (CHUNK, CHUNK), 0)
    dst = lax.broadcasted_iota(jnp.int32, (CHUNK, CHUNK), 1)
    prefix = jnp.where(src <= dst, 1.0, 0.0).astype(BF16)
    suffix = jnp.where(src >= dst, 1.0, 0.0).astype(BF16)
    o_ref[0, 0] = i_f
    o_ref[0, 1] = _split_dot(_log_sigmoid(f_f), prefix)
    o_ref[1, 0] = i_b
    o_ref[1, 1] = _split_dot(_log_sigmoid(f_b), suffix)


def _gate_prep(g_raw, bias):
    s = g_raw.shape[0]
    return pl.pallas_call(
        _gate_prep_kernel,
        out_shape=jax.ShapeDtypeStruct((2, 2, H_B, s), F32),
        grid=(s // CHUNK,),
        in_specs=[pl.BlockSpec((CHUNK, LANE), lambda c: (c, 0)),
                  pl.BlockSpec((1, LANE), lambda c: (0, 0))],
        out_specs=pl.BlockSpec((2, 2, H_B, CHUNK), lambda c: (0, 0, 0, c)),
        compiler_params=_params("parallel"),
        name="mlstm_gate_prep",
    )(g_raw, bias)


def _mlstm_kernel(*refs, reverse, finalize, dk, dv):
    if finalize:
        q_ref, k_ref, v_ref, gp_ref, hb_ref, o_ref, nrm_ref, out_ref, c_ref, n_ref, m_ref = refs
    else:
        q_ref, k_ref, v_ref, gp_ref, out_ref, c_ref, n_ref, m_ref = refs

    @pl.when(pl.program_id(0) == 0)
    def _():
        c_ref[...] = jnp.zeros_like(c_ref)
        n_ref[...] = jnp.zeros_like(n_ref)
        m_ref[...] = jnp.zeros_like(m_ref)

    L = CHUNK
    i_rows = gp_ref[0]
    b_rows = gp_ref[1]
    i_cols = jnp.transpose(i_rows)
    b_cols = jnp.transpose(b_rows)
    t_idx = lax.broadcasted_iota(jnp.int32, (L, L), 0)
    s_idx = lax.broadcasted_iota(jnp.int32, (L, L), 1)
    mask = (s_idx >= t_idx) if reverse else (s_idx <= t_idx)
    scale = dk ** -0.5

    for h in range(H_B):
        q = (q_ref[:, h * dk:(h + 1) * dk].astype(F32) * scale).astype(BF16)
        k = k_ref[:, h * dk:(h + 1) * dk]
        v = v_ref[:, h * dv:(h + 1) * dv]
        i_row, b_row = i_rows[h:h + 1, :], b_rows[h:h + 1, :]
        i_col, b_col = i_cols[:, h:h + 1], b_cols[:, h:h + 1]
        b_end = b_row[:, 0:1] if reverse else b_row[:, L - 1:L]
        c_prev = c_ref[h]
        n_prev = n_ref[h]
        m_prev = m_ref[h][0:1, 0:1]

        dlog = jnp.where(mask, b_col - b_row + i_row, -jnp.inf)
        m_inter = b_col + m_prev
        m_t = jnp.maximum(m_inter, jnp.max(dlog, axis=-1, keepdims=True))
        qk = lax.dot_general(q, k, (((1,), (1,)), ((), ())), preferred_element_type=F32)
        scores = qk * jnp.exp(dlog - m_t)
        w_inter = jnp.exp(m_inter - m_t)
        num = (jnp.dot(scores.astype(BF16), v, preferred_element_type=F32)
               + w_inter * jnp.dot(q, c_prev.astype(BF16), preferred_element_type=F32))
        den = (jnp.sum(scores, axis=-1, keepdims=True)
               + w_inter * jnp.sum(q.astype(F32) * n_prev, axis=-1, keepdims=True))
        hh = num / jnp.maximum(jnp.abs(den), jnp.exp(-m_t))

        if finalize:
            hs = hh + hb_ref[:, h * dv:(h + 1) * dv]
            hs = hs * lax.rsqrt(jnp.mean(hs * hs, axis=-1, keepdims=True) + RMS_EPS)
            hs = hs * nrm_ref[:, h * dv:(h + 1) * dv]
            gate = jax.nn.sigmoid(o_ref[:, h * dv:(h + 1) * dv].astype(F32))
            out_ref[:, h * dv:(h + 1) * dv] = (hs * gate).astype(out_ref.dtype)
        else:
            out_ref[:, h * dv:(h + 1) * dv] = hh

        m_loc = jnp.max(i_row + (b_end - b_row), axis=-1, keepdims=True)
        w_end = jnp.exp(i_col + (b_end - b_col) - m_loc)
        kw = k.astype(F32) * w_end
        c_loc = lax.dot_general(kw.astype(BF16), v, (((0,), (0,)), ((), ())),
                                preferred_element_type=F32)
        n_loc = jnp.sum(kw, axis=0, keepdims=True)
        m_new = jnp.maximum(b_end + m_prev, m_loc)
        dec = jnp.exp(b_end + m_prev - m_new)
        inc = jnp.exp(m_loc - m_new)
        c_ref[h] = dec * c_prev + inc * c_loc
        n_ref[h] = dec * n_prev + inc * n_loc
        m_ref[h] = jnp.broadcast_to(m_new, m_ref.shape[1:])


def _mlstm(z, gp, w_a, w_b, *, reverse, h_other=None, mh_norm=None):
    s = z.shape[0]
    nc = s // CHUNK
    dv = w_b // H_B
    dk = dv // 2
    qw, vw = H_B * dk, H_B * dv
    q0, k0, v0, o0 = 2 * w_a, 2 * w_a + qw, 2 * w_a + 2 * qw, 2 * w_a + 2 * qw + vw
    finalize = h_other is not None
    cidx = (lambda c: nc - 1 - c) if reverse else (lambda c: c)
    d = 1 if reverse else 0
    in_specs = [pl.BlockSpec((CHUNK, qw), lambda c: (cidx(c), q0 // qw)),
                pl.BlockSpec((CHUNK, qw), lambda c: (cidx(c), k0 // qw)),
                pl.BlockSpec((CHUNK, vw), lambda c: (cidx(c), v0 // vw)),
                pl.BlockSpec((None, 2, H_B, CHUNK), lambda c: (d, 0, 0, cidx(c)))]
    operands = [z, z, z, gp]
    if finalize:
        in_specs += [pl.BlockSpec((CHUNK, vw), lambda c: (cidx(c), 0)),
                     pl.BlockSpec((CHUNK, vw), lambda c: (cidx(c), o0 // vw)),
                     pl.BlockSpec((1, vw), lambda c: (0, 0))]
        operands += [h_other, z, mh_norm.reshape(1, vw).astype(F32)]
    kern = functools.partial(_mlstm_kernel, reverse=reverse, finalize=finalize, dk=dk, dv=dv)
    return pl.pallas_call(
        kern,
        out_shape=jax.ShapeDtypeStruct((s, vw), BF16 if finalize else F32),
        grid=(nc,),
        in_specs=in_specs,
        out_specs=pl.BlockSpec((CHUNK, vw), lambda c: (cidx(c), 0)),
        scratch_shapes=[pltpu.VMEM((H_B, dk, dv), F32),
                        pltpu.VMEM((H_B, 1, dk), F32),
                        pltpu.VMEM((H_B, 8, LANE), F32)],
        compiler_params=_params("arbitrary"),
        name="mlstm_bwd_scan" if reverse else "mlstm_fwd_scan",
    )(*operands)


def _token_mixers(x, norm_g, w_in, gate_bias, ln_g, ln_b, w_s, b_s, mh_norm, w_out):
    s, d = x.shape
    w_a = d // 2
    w_b = d - w_a
    n_main = w_in.shape[1] - N_GATES * H_B
    h = _rmsnorm(x, norm_g, BF16)
    w_main = w_in[:, :n_main].astype(BF16)
    w_gate = jnp.pad(w_in[:, n_main:], ((0, 0), (0, LANE - N_GATES * H_B))).astype(BF16)
    z = _matmul([h], [w_main], functools.partial(_epi_gelu_prefix, gelu_cols=2 * w_a), BF16,
                name="in_proj")
    g_raw = _matmul([h], [w_gate], _epi_plain, F32, name="in_proj_gates")
    bias = jnp.pad(gate_bias.astype(F32).reshape(1, N_GATES * H_B),
                   ((0, 0), (0, LANE - N_GATES * H_B)))
    gp = _gate_prep(g_raw, bias)
    out_a = _gmlp(z, ln_g, ln_b, w_s, b_s, w_a)
    h_bwd = _mlstm(z, gp, w_a, w_b, reverse=True)
    out_b = _mlstm(z, gp, w_a, w_b, reverse=False, h_other=h_bwd, mh_norm=mh_norm)
    return _matmul([out_a, out_b], [w_out.astype(BF16)], _epi_residual, F32, res=x, name="out_proj")


def _dense_ffn(x, norm_g, w_gate, w_up, w_down):
    h = _rmsnorm(x, norm_g, BF16)
    a = _matmul([h], [w_gate.astype(BF16), w_up.astype(BF16)], _epi_swiglu, BF16, tn=256,
                name="ffn_gate_up")
    return _matmul([a], [w_down.astype(BF16)], _epi_residual, F32, res=x, tm=512, tn=256,
                   name="ffn_down")


def _router_kernel(h_ref, r_ref, o_ref):
    logits = jnp.dot(h_ref[...].astype(BF16), r_ref[...], preferred_element_type=F32)
    lane = lax.broadcasted_iota(jnp.int32, logits.shape, 1)
    lane_f = lane.astype(F32)
    logits = jnp.where(lane < N_EXPERTS, logits, -jnp.inf)
    v1 = jnp.max(logits, axis=-1, keepdims=True)
    e1 = jnp.min(jnp.where(logits == v1, lane_f, float(LANE)), axis=-1, keepdims=True)
    rest = jnp.where(lane_f == e1, -jnp.inf, logits)
    v2 = jnp.max(rest, axis=-1, keepdims=True)
    e2 = jnp.min(jnp.where(rest == v2, lane_f, float(LANE)), axis=-1, keepdims=True)
    p2 = jnp.exp(v2 - v1)
    w1 = 1.0 / (1.0 + p2)
    w2 = p2 / (1.0 + p2)
    out = jnp.where(lane == 0, e1, 0.0)
    out = jnp.where(lane == 1, e2, out)
    out = jnp.where(lane == 2, w1, out)
    out = jnp.where(lane == 3, w2, out)
    o_ref[...] = out


def _router(h, router):
    s, d = h.shape
    tm = _tile(s, NORM_TM, 8)
    r = jnp.pad(router, ((0, 0), (0, LANE - N_EXPERTS))).astype(BF16)
    return pl.pallas_call(
        _router_kernel,
        out_shape=jax.ShapeDtypeStruct((s, LANE), F32),
        grid=(s // tm,),
        in_specs=[pl.BlockSpec((tm, d), lambda i: (i, 0)),
                  pl.BlockSpec((d, LANE), lambda i: (0, 0))],
        out_specs=pl.BlockSpec((tm, LANE), lambda i: (i, 0)),
        compiler_params=_params("parallel"),
        name="moe_router",
    )(h, r)


def _row_copy(src_ref, dst_ref, src_row, dst_row, n, sem):
    return pltpu.make_async_copy(src_ref.at[pl.ds(src_row, n)], dst_ref.at[pl.ds(dst_row, n)], sem)


def _gather_rows_kernel(idx_ref, nt_ref, src_ref, dst_ref, sem, *, tm):
    i = pl.program_id(0)

    @pl.when(i < nt_ref[0])
    def _():
        base = i * tm

        def issue(r, carry):
            _row_copy(src_ref, dst_ref, idx_ref[base + r], base + r, 1, sem).start()
            return carry

        lax.fori_loop(0, tm, issue, 0)
        _row_copy(src_ref, dst_ref, 0, base, tm, sem).wait()


def _gather_rows(src, idx, n_tiles, tm):
    rows = idx.shape[0]
    d = src.shape[1]
    return pl.pallas_call(
        functools.partial(_gather_rows_kernel, tm=tm),
        out_shape=jax.ShapeDtypeStruct((rows, d), src.dtype),
        grid_spec=pltpu.PrefetchScalarGridSpec(
            num_scalar_prefetch=2,
            grid=(rows // tm,),
            in_specs=[pl.BlockSpec(memory_space=pl.ANY)],
            out_specs=pl.BlockSpec(memory_space=pl.ANY),
            scratch_shapes=[pltpu.SemaphoreType.DMA(())]),
        compiler_params=_params("arbitrary"),
        name="moe_gather_rows",
    )(idx, n_tiles, src)


def _expert_up_kernel(te_ref, nt_ref, x_ref, wg_ref, wu_ref, o_ref):
    @pl.when(pl.program_id(1) < nt_ref[0])
    def _():
        x = x_ref[...].astype(BF16)
        g = jnp.dot(x, wg_ref[...], preferred_element_type=F32)
        u = jnp.dot(x, wu_ref[...], preferred_element_type=F32)
        o_ref[...] = (jax.nn.silu(g) * u).astype(o_ref.dtype)


def _expert_down_kernel(te_ref, nt_ref, a_ref, w_ref, o_ref):
    @pl.when(pl.program_id(1) < nt_ref[0])
    def _():
        o_ref[...] = jnp.dot(a_ref[...], w_ref[...], preferred_element_type=F32)


def _expert_matmul(kernel, x, w_list, tile_expert, n_tiles, tm, tn, out_dtype, name):
    rows, k = x.shape
    n = w_list[0].shape[2]
    tn = _tile(n, tn, LANE)
    row_blk = lambda j, i, te, nt: (jnp.minimum(i, nt[0] - 1), 0)
    w_blk = lambda j, i, te, nt: (te[i], 0, j)
    out_blk = lambda j, i, te, nt: (jnp.minimum(i, nt[0] - 1), j)
    return pl.pallas_call(
        kernel,
        out_shape=jax.ShapeDtypeStruct((rows, n), out_dtype),
        grid_spec=pltpu.PrefetchScalarGridSpec(
            num_scalar_prefetch=2,
            grid=(n // tn, rows // tm),
            in_specs=[pl.BlockSpec((tm, k), row_blk)]
                     + [pl.BlockSpec((None, k, tn), w_blk) for _ in w_list],
            out_specs=pl.BlockSpec((tm, tn), out_blk)),
        compiler_params=_params("arbitrary", "arbitrary"),
        name=name,
    )(tile_expert, n_tiles, x, *w_list)


def _combine_kernel(d1_ref, d2_ref, x_ref, rw_ref, g_ref, y_ref, o_ref, y1_buf, y2_buf, sem,
                    *, tm):
    base = pl.program_id(0) * tm

    def issue(r, carry):
        _row_copy(y_ref, y1_buf, d1_ref[base + r], r, 1, sem.at[0]).start()
        _row_copy(y_ref, y2_buf, d2_ref[base + r], r, 1, sem.at[1]).start()
        return carry

    lax.fori_loop(0, tm, issue, 0)
    _row_copy(y_ref, y1_buf, 0, 0, tm, sem.at[0]).wait()
    _row_copy(y_ref, y2_buf, 0, 0, tm, sem.at[1]).wait()
    rw = rw_ref[...]
    x = x_ref[...] + rw[:, 2:3] * y1_buf[...] + rw[:, 3:4] * y2_buf[...]
    ms = jnp.mean(x * x, axis=-1, keepdims=True)
    o_ref[...] = x * lax.rsqrt(ms + RMS_EPS) * g_ref[...]


def _combine(x, route, y, dest1, dest2, final_g):
    s, d = x.shape
    tm = _tile(s, NORM_TM, 8)
    return pl.pallas_call(
        functools.partial(_combine_kernel, tm=tm),
        out_shape=jax.ShapeDtypeStruct((s, d), F32),
        grid_spec=pltpu.PrefetchScalarGridSpec(
            num_scalar_prefetch=2,
            grid=(s // tm,),
            in_specs=[pl.BlockSpec((tm, d), lambda i, d1, d2: (i, 0)),
                      pl.BlockSpec((tm, LANE), lambda i, d1, d2: (i, 0)),
                      pl.BlockSpec((1, d), lambda i, d1, d2: (0, 0)),
                      pl.BlockSpec(memory_space=pl.ANY)],
            out_specs=pl.BlockSpec((tm, d), lambda i, d1, d2: (i, 0)),
            scratch_shapes=[pltpu.VMEM((tm, d), F32), pltpu.VMEM((tm, d), F32),
                            pltpu.SemaphoreType.DMA((2,))]),
        compiler_params=_params("arbitrary"),
        name="moe_combine_final_norm",
    )(dest1, dest2, x, route, final_g.reshape(1, d).astype(F32), y)


def _moe_final(x, norm_g, router, w_gate, w_up, w_down, final_g):
    s, d = x.shape
    h = _rmsnorm(x, norm_g, F32)
    route = _router(h, router)

    tm = _tile(TOP_K * s, MOE_TM, 8)
    n_slots = TOP_K * s // tm + N_EXPERTS
    e_flat = route[:, :TOP_K].astype(jnp.int32).reshape(-1)
    onehot = (e_flat[:, None] == jnp.arange(N_EXPERTS, dtype=jnp.int32)[None, :]).astype(jnp.int32)
    rank = jnp.sum((jnp.cumsum(onehot, axis=0) - onehot) * onehot, axis=1)
    counts = jnp.sum(onehot, axis=0)
    tiles_e = (counts + tm - 1) // tm
    tile_end = jnp.cumsum(tiles_e)
    n_tiles = tile_end[-1:].astype(jnp.int32)
    dest = ((tile_end - tiles_e) * tm)[e_flat] + rank
    tile_ids = jnp.minimum(jnp.arange(n_slots, dtype=jnp.int32), n_tiles[0] - 1)
    tile_expert = jnp.sum(tile_ids[:, None] >= tile_end[None, :], axis=1).astype(jnp.int32)
    src_tok = jnp.zeros((n_slots * tm,), jnp.int32).at[dest].set(
        jnp.arange(TOP_K * s, dtype=jnp.int32) // TOP_K)
    dest = dest.reshape(s, TOP_K).astype(jnp.int32)

    xs = _gather_rows(h, src_tok, n_tiles, tm)
    a = _expert_matmul(_expert_up_kernel, xs, [w_gate.astype(BF16), w_up.astype(BF16)],
                       tile_expert, n_tiles, tm, 512, BF16, "moe_expert_gate_up")
    y = _expert_matmul(_expert_down_kernel, a, [w_down.astype(BF16)],
                       tile_expert, n_tiles, tm, 512, F32, "moe_expert_down")
    return _combine(x, route, y, dest[:, 0], dest[:, 1], final_g)


def kernel(x, norm_mix, w_in, gate_bias, ln_v_gain, ln_v_bias, w_s, b_s, mh_norm, w_out, norm_ffn,
           ffn_w_gate, ffn_w_up, ffn_w_down, router, moe_w_gate, moe_w_up, moe_w_down, final_norm):
    b, s, d = x.shape
    depth = norm_mix.shape[0]
    assert b == 1 and depth == 2 and s % CHUNK == 0
    xs = x.reshape(s, d)
    xs = _token_mixers(xs, norm_mix[0], w_in[0], gate_bias[0], ln_v_gain[0], ln_v_bias[0],
                       w_s[0], b_s[0], mh_norm[0], w_out[0])
    xs = _dense_ffn(xs, norm_ffn[0], ffn_w_gate[0], ffn_w_up[0], ffn_w_down[0])
    xs = _token_mixers(xs, norm_mix[1], w_in[1], gate_bias[1], ln_v_gain[1], ln_v_bias[1],
                       w_s[1], b_s[1], mh_norm[1], w_out[1])
    out = _moe_final(xs, norm_ffn[1], router[0], moe_w_gate[0], moe_w_up[0], moe_w_down[0],
                     final_norm)
    return out.reshape(b, s, d)
```
